```python
import jax, jax.numpy as jnp
from jax import lax
import numpy as np

D_MODEL = 1024
BATCH = 16
SEQ = 256
DEPTH = 2
DEC_BATCH = 2
DEC_SEQ = 4096
PAST_LEN = 256

GRID_W = 64
CHUNK = GRID_W
N_HEADS = 4
HEAD_DIM = 256
D_MLSTM = N_HEADS * HEAD_DIM
N_FGROUPS = 4
FGROUP_DIM = 128
D_FOURIER = N_FGROUPS * FGROUP_DIM
N_EXPERTS = 16
N_EGROUPS = 4
EXPERTS_PER_GROUP = N_EXPERTS // N_EGROUPS
TOP_K = 2
D_EXPERT = 512
ALPHA = (2 * DEPTH) ** 0.25
BETA = (8 * DEPTH) ** -0.25
LN_EPS = 1e-6
SPLIT_POINTS = (D_MLSTM, 2 * D_MLSTM, 3 * D_MLSTM, 4 * D_MLSTM, 4 * D_MLSTM + D_FOURIER,
                4 * D_MLSTM + D_FOURIER + D_MODEL, 4 * D_MLSTM + D_FOURIER + 2 * D_MODEL)
GATE_OFFSET = 4 * D_MLSTM + D_FOURIER + 2 * D_MODEL
N_IN = GATE_OFFSET + 4 * N_HEADS

kernel_name = "mlstm_fnet_gated_moe_diffusion_step"


def layer_norm(x, g, b):
    xf = x.astype(jnp.float32)
    mu = jnp.mean(xf, axis=-1, keepdims=True)
    var = jnp.mean(jnp.square(xf - mu), axis=-1, keepdims=True)
    return ((xf - mu) * lax.rsqrt(var + LN_EPS) * g + b).astype(x.dtype)


def mlstm_scan(q, k, v, i_pre, log_f, C0, n0, m0):
    B, Hd, L, Dh = q.shape
    n_chunks = L // CHUNK

    def to_chunks(a):
        return jnp.moveaxis(a.reshape(B, Hd, n_chunks, CHUNK, *a.shape[3:]), 2, 0)

    causal = jnp.tril(jnp.ones((CHUNK, CHUNK), dtype=bool))

    def step(carry, xs):
        C, n, m = carry
        qc, kc, vc, ic, fc = xs
        b = jnp.cumsum(fc, axis=-1)
        log_d = jnp.where(causal, b[..., :, None] - b[..., None, :] + ic[..., None, :], -jnp.inf)
        log_inter = b + m[..., None]
        m_t = jnp.maximum(log_inter, jnp.max(log_d, axis=-1))
        s = jnp.einsum('bhtd,bhsd->bhts', qc, kc) * jnp.exp(log_d - m_t[..., None])
        w_inter = jnp.exp(log_inter - m_t)
        num = jnp.einsum('bhts,bhsd->bhtd', s, vc) + w_inter[..., None] * jnp.einsum('bhvk,bhtk->bhtv', C, qc)
        den = jnp.sum(s, axis=-1) + w_inter * jnp.einsum('bhk,bhtk->bht', n, qc)
        h = num / jnp.maximum(jnp.abs(den), jnp.exp(-m_t))[..., None]
        b_last = b[..., -1]
        log_w = b_last[..., None] - b + ic
        m_new = jnp.maximum(b_last + m, jnp.max(log_w, axis=-1))
        w = jnp.exp(log_w - m_new[..., None])
        decay = jnp.exp(b_last + m - m_new)
        C_new = decay[..., None, None] * C + jnp.einsum('bhsv,bhsk->bhvk', vc * w[..., None], kc)
        n_new = decay[..., None] * n + jnp.einsum('bhs,bhsk->bhk', w, kc)
        return (C_new, n_new, m_new), h

    xs = (to_chunks(q), to_chunks(k), to_chunks(v), to_chunks(i_pre), to_chunks(log_f))
    (C, n, m), h = lax.scan(step, (C0, n0, m0), xs)
    h = jnp.moveaxis(h, 0, 2).reshape(B, Hd, L, Dh)
    return h, (C, n, m)


def grouped_top2_moe(h, w_router, b_router, w_gate, w_up, w_down):
    probs = jax.nn.softmax((h @ w_router).astype(jnp.float32) + b_router, axis=-1)
    grp = probs.reshape(*probs.shape[:-1], N_EGROUPS, EXPERTS_PER_GROUP)
    top_v, top_i = lax.top_k(grp, TOP_K)
    best = jnp.argmax(jnp.sum(top_v, axis=-1), axis=-1)
    w = top_v / jnp.sum(top_v, axis=-1, keepdims=True)
    in_group = jnp.einsum('...gk,...gke->...ge', w, jax.nn.one_hot(top_i, EXPERTS_PER_GROUP, dtype=jnp.float32))
    combine = (jax.nn.one_hot(best, N_EGROUPS, dtype=jnp.float32)[..., None] * in_group)
    combine = combine.reshape(*probs.shape).astype(h.dtype)
    y = jnp.zeros_like(h)
    for e in range(N_EXPERTS):
        a = jax.nn.silu(h @ w_gate[e]) * (h @ w_up[e])
        y = y + combine[..., e:e + 1] * (a @ w_down[e])
    return y


def trunk_layer(x, mod, C0, n0, m0, w_in, b_in, w_a, w_b, w_out, g_head,
                ln1_g, ln1_b, ln2_g, ln2_b, w_router, b_router, w_gate, w_up, w_down):
    B, L, _ = x.shape
    shift1, scale1, gate1, shift2, scale2, gate2 = jnp.split(mod, 6, axis=-1)
    h = x * (1 + scale1) + shift1
    z = h @ w_in + b_in
    q, k, v, o, u, g_a, g_b, gates = jnp.split(z, SPLIT_POINTS, axis=-1)

    def heads(a):
        return a.astype(jnp.float32).reshape(B, L, N_HEADS, HEAD_DIM).transpose(0, 2, 1, 3)
    qh, kh, vh = heads(q), heads(k) * (HEAD_DIM ** -0.5), heads(v)
    gates = gates.astype(jnp.float32).reshape(B, L, 2, 2, N_HEADS).transpose(2, 3, 0, 4, 1)
    i_pre, log_f = gates[0], jax.nn.log_sigmoid(gates[1])

    def both(fwd, bwd):
        return jnp.concatenate([fwd, jnp.flip(bwd, axis=2)], axis=1)
    hs, state = mlstm_scan(both(qh, qh), both(kh, kh), both(vh, vh),
                           both(i_pre[0], i_pre[1]), both(log_f[0], log_f[1]), C0, n0, m0)
    hsum = (hs[:, :N_HEADS] + jnp.flip(hs[:, N_HEADS:], axis=2)).transpose(0, 2, 1, 3)
    mu = jnp.mean(hsum, axis=-1, keepdims=True)
    var = jnp.mean(jnp.square(hsum - mu), axis=-1, keepdims=True)
    hn = ((hsum - mu) * lax.rsqrt(var + LN_EPS)).reshape(B, L, D_MLSTM) * g_head
    y_a = (jax.nn.sigmoid(o.astype(jnp.float32)) * hn).astype(x.dtype) @ w_a

    uf = u.astype(jnp.float32).reshape(B, L, N_FGROUPS, FGROUP_DIM)
    y_f = jnp.real(jnp.fft.fft2(uf, axes=(1, 3), norm="ortho")).reshape(B, L, D_FOURIER).astype(x.dtype)
    y_b = y_f @ w_b

    merged = jax.nn.sigmoid(g_a) * y_a + jax.nn.sigmoid(g_b) * y_b
    x = layer_norm(ALPHA * x + gate1 * (merged @ w_out), ln1_g, ln1_b)

    h2 = x * (1 + scale2) + shift2
    y2 = grouped_top2_moe(h2, w_router, b_router, w_gate, w_up, w_down)
    x = layer_norm(ALPHA * x + gate2 * y2, ln2_g, ln2_b)
    return x, state


def setup_inputs(seed: int = 0) -> dict:
    key = jax.random.key(seed)
    ks = jax.random.split(key, 26)
    nrm = lambda k, shape, s: jax.random.normal(k, shape, jnp.float32) * s
    b_in = nrm(ks[8], (DEPTH, N_IN), 0.02)
    f_bias = jax.random.uniform(ks[9], (DEPTH, 2 * N_HEADS), jnp.float32, 3.0, 6.0)
    b_in = b_in.at[:, GATE_OFFSET + 2 * N_HEADS:].add(f_bias)
    return {
        "x_prompt": nrm(ks[0], (BATCH, SEQ, D_MODEL), 1.0),
        "x_sample": nrm(ks[1], (DEC_BATCH, DEC_SEQ, D_MODEL), 1.0),
        "state_C": nrm(ks[2], (DEC_BATCH, DEPTH, 2, N_HEADS, HEAD_DIM, HEAD_DIM), 0.1),
        "state_n": nrm(ks[3], (DEC_BATCH, DEPTH, 2, N_HEADS, HEAD_DIM), 0.1),
        "state_m": nrm(ks[4], (DEC_BATCH, DEPTH, 2, N_HEADS), 1.0),
        "c": nrm(ks[5], (DEC_BATCH, D_MODEL), 1.0),
        "c_ctx": nrm(ks[6], (D_MODEL,), 1.0),
        "w_in": nrm(ks[7], (DEPTH, D_MODEL, N_IN), D_MODEL ** -0.5),
        "b_in": b_in,
        "w_a": nrm(ks[10], (DEPTH, D_MLSTM, D_MODEL), D_MLSTM ** -0.5),
        "w_b": nrm(ks[11], (DEPTH, D_FOURIER, D_MODEL), D_FOURIER ** -0.5),
        "w_out": nrm(ks[12], (DEPTH, D_MODEL, D_MODEL), BETA * D_MODEL ** -0.5),
        "g_head": 1.0 + nrm(ks[13], (DEPTH, D_MLSTM), 0.02),
        "ln1_g": 1.0 + nrm(ks[14], (DEPTH, D_MODEL), 0.02),
        "ln1_b": nrm(ks[15], (DEPTH, D_MODEL), 0.02),
        "ln2_g": 1.0 + nrm(ks[16], (DEPTH, D_MODEL), 0.02),
        "ln2_b": nrm(ks[17], (DEPTH, D_MODEL), 0.02),
        "w_ada": nrm(ks[18], (DEPTH, D_MODEL, 6 * D_MODEL), 0.5 * D_MODEL ** -0.5),
        "b_ada": nrm(ks[19], (DEPTH, 6 * D_MODEL), 0.02),
        "w_router": nrm(ks[20], (D_MODEL, N_EXPERTS), D_MODEL ** -0.5),
        "b_router": nrm(ks[21], (N_EXPERTS,), 0.01),
        "w_gate": nrm(ks[22], (DEPTH, N_EXPERTS, D_MODEL, D_EXPERT), D_MODEL ** -0.5),
        "w_up": nrm(ks[23], (DEPTH, N_EXPERTS, D_MODEL, D_EXPERT), D_MODEL ** -0.5),
        "w_down": nrm(ks[24], (DEPTH, N_EXPERTS, D_EXPERT, D_MODEL), BETA * D_EXPERT ** -0.5),
    }


def reference(x_prompt, x_sample, state_C, state_n, state_m, c, c_ctx, w_in, b_in, w_a, w_b, w_out,
              g_head, ln1_g, ln1_b, ln2_g, ln2_b, w_ada, b_ada, w_router, b_router, w_gate, w_up, w_down):
    H2 = 2 * N_HEADS

    def layer_weights(l):
        return (w_in[l], b_in[l], w_a[l], w_b[l], w_out[l], g_head[l], ln1_g[l], ln1_b[l],
                ln2_g[l], ln2_b[l], w_router, b_router, w_gate[l], w_up[l], w_down[l])

    B = x_prompt.shape[0]
    C0 = jnp.zeros((B, H2, HEAD_DIM, HEAD_DIM), jnp.float32)
    n0 = jnp.zeros((B, H2, HEAD_DIM), jnp.float32)
    m0 = jnp.zeros((B, H2), jnp.float32)
    xp = x_prompt
    new_C, new_n, new_m = [], [], []
    for l in range(DEPTH):
        mod_ctx = (jax.nn.silu(c_ctx) @ w_ada[l] + b_ada[l]).reshape(1, 1, 6 * D_MODEL)
        xp, (C_l, n_l, m_l) = trunk_layer(xp, mod_ctx, C0, n0, m0, *layer_weights(l))
        new_C.append(C_l.reshape(B, 2, N_HEADS, HEAD_DIM, HEAD_DIM))
        new_n.append(n_l.reshape(B, 2, N_HEADS, HEAD_DIM))
        new_m.append(m_l.reshape(B, 2, N_HEADS))
    new_state_C = jnp.stack(new_C, axis=1).astype(x_prompt.dtype)
    new_state_n = jnp.stack(new_n, axis=1).astype(x_prompt.dtype)
    new_state_m = jnp.stack(new_m, axis=1).astype(x_prompt.dtype)

    Bd = x_sample.shape[0]
    xs = x_sample
    for l in range(DEPTH):
        mod_lat = (jax.nn.silu(c) @ w_ada[l] + b_ada[l])[:, None, :]
        Cl = state_C[:, l].reshape(Bd, H2, HEAD_DIM, HEAD_DIM).astype(jnp.float32)
        nl = state_n[:, l].reshape(Bd, H2, HEAD_DIM).astype(jnp.float32)
        ml = state_m[:, l].reshape(Bd, H2).astype(jnp.float32)
        xs, _ = trunk_layer(xs, mod_lat, Cl, nl, ml, *layer_weights(l))

    return (xp, xs, new_state_C, new_state_n, new_state_m)
```

```python
import functools

import numpy as np
import jax
import jax.numpy as jnp
from jax import lax
from jax.experimental import pallas as pl
from jax.experimental.pallas import tpu as pltpu

F32 = jnp.float32
BF16 = jnp.bfloat16
HIGHEST = lax.Precision.HIGHEST

D = 1024
N_CTX_SEQ, CTX_LEN = 16, 256
N_LAT_SEQ, LAT_LEN = 2, 4096
N_CTX_TOK = N_CTX_SEQ * CTX_LEN
NT = N_CTX_TOK + N_LAT_SEQ * LAT_LEN
MOD_ROWS = 8
TOK_PER_MOD = 4096
NH, DH = 4, 256
T = 256
NG = 16
NE, EG, DE = 16, 4, 512
DF, FG = 512, 128
ALPHA = 4.0 ** 0.25
LN_EPS = 1e-6
ZQ, ZK, ZV, ZO, ZGA, ZGB, ZU = 0, 1024, 2048, 3072, 4096, 5120, 6144
NZ = 6656
VMEM_LIMIT = 56 * 1024 * 1024
SLOT = 128
MOE_TM = 1024
S_MAX = 2 * MOE_TM + NE * SLOT


def _cp(sem, vmem=VMEM_LIMIT):
    return pltpu.CompilerParams(dimension_semantics=sem, vmem_limit_bytes=vmem)


def _nt_dot(a, b, precision=None):
    return lax.dot_general(a, b, (((1,), (1,)), ((), ())), precision=precision,
                           preferred_element_type=F32)


def _dot(a, b, precision=None):
    return jnp.dot(a, b, precision=precision, preferred_element_type=F32)


def _sigmoid(x):
    return 1.0 / (1.0 + jnp.exp(-x))


def _mod_row(mod_ref, tile, tm, part):
    r = (tile * tm) // TOK_PER_MOD
    return mod_ref[0, pl.ds(r, 1), part * D:(part + 1) * D]


def _layer_norm(r, g, b):
    mu = jnp.mean(r, axis=-1, keepdims=True)
    rc = r - mu
    var = jnp.mean(rc * rc, axis=-1, keepdims=True)
    return rc * lax.rsqrt(var + LN_EPS) * g + b


def _mod_kernel(cond_ref, w_ref, b_ref, out_ref):
    cnd = cond_ref[...]
    s = cnd * _sigmoid(cnd)
    out_ref[0] = _dot(s, w_ref[0], HIGHEST) + b_ref[0]


def _modulation(cond, w_ada, b_ada):
    depth = w_ada.shape[0]
    tn = 1024
    return pl.pallas_call(
        _mod_kernel,
        grid=(depth, 6 * D // tn),
        in_specs=[pl.BlockSpec((MOD_ROWS, D), lambda l, j: (0, 0)),
                  pl.BlockSpec((1, D, tn), lambda l, j: (l, 0, j)),
                  pl.BlockSpec((1, 1, tn), lambda l, j: (l, 0, j))],
        out_specs=pl.BlockSpec((1, MOD_ROWS, tn), lambda l, j: (l, 0, j)),
        out_shape=jax.ShapeDtypeStruct((depth, MOD_ROWS, 6 * D), F32),
        compiler_params=_cp(("arbitrary", "arbitrary")),
        name="modulation",
    )(cond, w_ada, b_ada.reshape(depth, 1, 6 * D))


def _prep_kernel(x_ref, mod_ref, wg_ref, bg_ref, h_ref, gt_ref, *, tm):
    i = pl.program_id(0)
    shift = _mod_row(mod_ref, i, tm, 0)
    scale = _mod_row(mod_ref, i, tm, 1)
    h = x_ref[...] * (1.0 + scale) + shift
    h_ref[...] = h.astype(BF16)
    gt_ref[...] = _nt_dot(wg_ref[...], h, HIGHEST) + bg_ref[...]


def _prep(x, mod, layer, wg_t, bg):
    tm = 512
    return pl.pallas_call(
        functools.partial(_prep_kernel, tm=tm),
        grid=(NT // tm,),
        in_specs=[pl.BlockSpec((tm, D), lambda i: (i, 0)),
                  pl.BlockSpec((1, MOD_ROWS, 6 * D), lambda i: (layer, 0, 0)),
                  pl.BlockSpec((NG, D), lambda i: (0, 0)),
                  pl.BlockSpec((NG, 1), lambda i: (0, 0))],
        out_specs=[pl.BlockSpec((tm, D), lambda i: (i, 0)),
                   pl.BlockSpec((NG, tm), lambda i: (0, i))],
        out_shape=[jax.ShapeDtypeStruct((NT, D), BF16),
                   jax.ShapeDtypeStruct((NG, NT), F32)],
        compiler_params=_cp(("arbitrary",)),
        name="prep",
    )(x, mod, wg_t, bg)


def _proj_kernel(h_ref, w_ref, b_ref, z_ref):
    z_ref[...] = (_dot(h_ref[...], w_ref[...]) + b_ref[...]).astype(BF16)


def _proj(h, w, b):
    tm, tn = 1024, 512
    return pl.pallas_call(
        _proj_kernel,
        grid=(NZ // tn, NT // tm),
        in_specs=[pl.BlockSpec((tm, D), lambda j, i: (i, 0)),
                  pl.BlockSpec((D, tn), lambda j, i: (0, j)),
                  pl.BlockSpec((1, tn), lambda j, i: (0, j))],
        out_specs=pl.BlockSpec((tm, tn), lambda j, i: (i, j)),
        out_shape=jax.ShapeDtypeStruct((NT, NZ), BF16),
        compiler_params=_cp(("arbitrary", "arbitrary")),
        name="proj",
    )(h, w, b)


def _projT_kernel(h_ref, wt_ref, b_ref, o_ref):
    o_ref[...] = (_nt_dot(wt_ref[...], h_ref[...]) + b_ref[...]).astype(BF16)


def _proj_vT(h, wv_t, bv_col):
    tm = 1024
    return pl.pallas_call(
        _projT_kernel,
        grid=(NT // tm,),
        in_specs=[pl.BlockSpec((tm, D), lambda i: (i, 0)),
                  pl.BlockSpec((D, D), lambda i: (0, 0)),
                  pl.BlockSpec((D, 1), lambda i: (0, 0))],
        out_specs=pl.BlockSpec((D, tm), lambda i: (0, i)),
        out_shape=jax.ShapeDtypeStruct((D, NT), BF16),
        compiler_params=_cp(("arbitrary",)),
        name="proj_vT",
    )(h, wv_t, bv_col)


def _lane_scan(x, op, ident, backward):
    n = x.shape[1]
    lane = lax.broadcasted_iota(jnp.int32, x.shape, 1)
    k = 1
    while k < n:
        if backward:
            cand = jnp.where(lane < n - k, pltpu.roll(x, n - k, 1), ident)
        else:
            cand = jnp.where(lane >= k, pltpu.roll(x, k, 1), ident)
        x = op(x, cand)
        k *= 2
    return x


N_Q = 6


def _gate_prep_kernel(g_ref, m0_ref, sel_ref, grow_ref, rcol_ref, mfin_ref, qf_scr, qb_scr, *, n_chunks):
    def one_pass(backward, scr):
        def step(j, m_prev):
            c = (n_chunks - 1 - j) if backward else j
            sl = pl.ds(pl.multiple_of(c * T, T), T)
            gi = g_ref[0:8, sl]
            gf = g_ref[8:16, sl]
            lf = jnp.minimum(gf, 0.0) - jnp.log1p(jnp.exp(-jnp.abs(gf)))
            b = _lane_scan(lf, jnp.add, 0.0, backward)
            a = gi - b
            big_a = jnp.maximum(m_prev, _lane_scan(a, jnp.maximum, -jnp.inf, backward))
            mt = b + big_a
            last = 0 if backward else T - 1
            a_end = big_a[:, last:last + 1]
            m_next = mt[:, last:last + 1]
            scr[0, c] = a
            scr[1, c] = jnp.exp(a - a_end)
            scr[2, c] = big_a
            scr[3, c] = jnp.exp(m_prev - big_a)
            scr[4, c] = jnp.exp(-mt)
            scr[5, c] = jnp.broadcast_to(jnp.exp(m_prev - a_end), (8, T))
            return m_next
        return lax.fori_loop(0, n_chunks, step, m0_ref[0, :, 0:1])

    m_f = one_pass(False, qf_scr)
    m_b = one_pass(True, qb_scr)
    is_f = lax.broadcasted_iota(jnp.int32, (8, T), 0) < 4
    zeros = jnp.zeros((128 - 8 * (N_Q - 2), T), F32)

    def emit(c, carry):
        sl = pl.ds(pl.multiple_of(c * T, T), T)
        q = [jnp.where(is_f, qf_scr[k, c], qb_scr[k, c]) for k in range(N_Q)]
        grow_ref[0:8, sl] = q[0]
        grow_ref[8:16, sl] = q[1]
        stack = jnp.concatenate(q[2:] + [zeros], axis=0)
        rcol_ref[sl, :] = _dot(stack.T, sel_ref[...], HIGHEST)
        return carry
    lax.fori_loop(0, n_chunks, emit, 0)
    is_f8 = lax.broadcasted_iota(jnp.int32, (8, 128), 0) < 4
    mfin_ref[0] = jnp.where(is_f8, jnp.broadcast_to(m_f, (8, 128)), jnp.broadcast_to(m_b, (8, 128)))


def _gate_select_matrix():
    sel = np.zeros((128, NH * 128), np.float32)
    for q in range(N_Q - 2):
        for d in range(2):
            for h in range(NH):
                sel[q * 8 + d * 4 + h, h * 128 + q * 2 + d] = 1.0
    return jnp.asarray(sel)


def _gate_prep(gates_t, m0, n_seq, seq_len, tok_off):
    n_chunks = seq_len // T
    off = tok_off // seq_len
    return pl.pallas_call(
        functools.partial(_gate_prep_kernel, n_chunks=n_chunks),
        grid=(n_seq,),
        in_specs=[pl.BlockSpec((NG, seq_len), lambda s: (0, s + off)),
                  pl.BlockSpec((1, 8, 128), lambda s: (s, 0, 0)),
                  pl.BlockSpec((128, NH * 128), lambda s: (0, 0))],
        out_specs=[pl.BlockSpec((NG, seq_len), lambda s: (0, s)),
                   pl.BlockSpec((seq_len, NH * 128), lambda s: (s, 0)),
                   pl.BlockSpec((1, 8, 128), lambda s: (s, 0, 0))],
        out_shape=[jax.ShapeDtypeStruct((NG, n_seq * seq_len), F32),
                   jax.ShapeDtypeStruct((n_seq * seq_len, NH * 128), F32),
                   jax.ShapeDtypeStruct((n_seq, 8, 128), F32)],
        scratch_shapes=[pltpu.VMEM((N_Q, n_chunks, 8, T), F32),
                        pltpu.VMEM((N_Q, n_chunks, 8, T), F32)],
        compiler_params=_cp(("arbitrary",)),
        name="gate_prep",
    )(gates_t, m0, _gate_select_matrix())


def _mlstm_kernel(*refs, n_chunks, has_init, want_final):
    it = iter(refs)
    q_ref, k_ref, v_ref, o_ref, vt_ref, grow_ref, rcol_ref, gh_ref = (next(it) for _ in range(8))
    c0_ref = n0_ref = None
    if has_init:
        c0_ref, n0_ref = next(it), next(it)
    next(it)
    out_ref = next(it)
    cfin_ref = nfin_ref = None
    if want_final:
        cfin_ref, nfin_ref = next(it), next(it)
    cb_scr, nb_scr, crun_scr, nrun_scr = (next(it) for _ in range(4))
    head = pl.program_id(1)

    def chunk_slice(c):
        return pl.ds(pl.multiple_of(c * T, T), T)

    for d in range(2):
        if has_init:
            crun_scr[...] = c0_ref[0, 0, d, 0]
            nrun_scr[...] = jnp.broadcast_to(n0_ref[0, 0, d, 0], (8, DH))
        else:
            crun_scr[...] = jnp.zeros((DH, DH), F32)
            nrun_scr[...] = jnp.zeros((8, DH), F32)
        n_updates = n_chunks if want_final else n_chunks - 1

        def state_step(j, carry, d=d):
            c = (n_chunks - 1 - j) if d == 1 else j
            sl = chunk_slice(c)
            cb_scr[d, c] = crun_scr[...].astype(BF16)
            nb_scr[d, c] = nrun_scr[...].astype(BF16)

            @pl.when(j < n_updates)
            def _():
                wrow = grow_ref[pl.ds(8 + d * 4 + head, 1), sl]
                decay = rcol_ref[pl.ds(pl.multiple_of(c * T, T), 1), 6 + d:7 + d]
                kc = k_ref[sl, :] * jnp.asarray(DH ** -0.5, BF16)
                vw = (vt_ref[:, sl].astype(F32) * wrow).astype(BF16)
                crun_scr[...] = decay * crun_scr[...] + _dot(vw, kc)
                w8 = jnp.broadcast_to(wrow, (8, T)).astype(BF16)
                nrun_scr[...] = decay * nrun_scr[...] + _dot(w8, kc)
            return carry
        lax.fori_loop(0, n_chunks, state_step, 0)
        if want_final:
            cfin_ref[0, d, 0] = crun_scr[...]
            nfin_ref[0, d, 0] = nrun_scr[0:1, :]

    row = lax.broadcasted_iota(jnp.int32, (T, T), 0)
    col = lax.broadcasted_iota(jnp.int32, (T, T), 1)
    gh = gh_ref[...]

    def out_step(c, carry):
        sl = chunk_slice(c)
        qc = q_ref[sl, :]
        kc = k_ref[sl, :] * jnp.asarray(DH ** -0.5, BF16)
        s = _nt_dot(qc, kc)
        rc = rcol_ref[sl, :]
        qf = qc.astype(F32)
        p = jnp.zeros((T, T), F32)
        num = jnp.zeros((T, DH), F32)
        for d in range(2):
            a_row = grow_ref[pl.ds(d * 4 + head, 1), sl]
            big_a = rc[:, d:d + 1]
            winter = rc[:, 2 + d:3 + d]
            floor = rc[:, 4 + d:5 + d]
            causal = (col <= row) if d == 0 else (col >= row)
            e = jnp.exp(jnp.where(causal, a_row - big_a, -jnp.inf))
            se = s * e
            den = jnp.sum(se, axis=1, keepdims=True)
            if has_init:
                nq = _nt_dot(qc, nb_scr[d, c])[:, 0:1]
                den = den + winter * nq
            r = 1.0 / jnp.maximum(jnp.abs(den), floor)
            p = p + se * r
            if has_init:
                qs = (qf * (winter * r)).astype(BF16)
                num = num + _nt_dot(qs, cb_scr[d, c])
        num = num + _dot(p.astype(BF16), v_ref[sl, :])
        mu = jnp.mean(num, axis=1, keepdims=True)
        nc = num - mu
        var = jnp.mean(nc * nc, axis=1, keepdims=True)
        hn = nc * lax.rsqrt(var + LN_EPS) * gh
        out_ref[sl, :] = (_sigmoid(o_ref[sl, :].astype(F32)) * hn).astype(BF16)
        return carry
    lax.fori_loop(0, n_chunks, out_step, 0)


def _mlstm(z, vt, grow, rcol, g_head, a_buf, n_seq, seq_len, tok_off, state=None):
    n_chunks = seq_len // T
    off = tok_off // seq_len
    has_init = state is not None
    want_final = not has_init

    def zspec(col0):
        return pl.BlockSpec((seq_len, DH), lambda b, h: (b + off, col0 // DH + h))
    in_specs = [zspec(ZQ), zspec(ZK), zspec(ZV), zspec(ZO),
                pl.BlockSpec((DH, seq_len), lambda b, h: (h, b + off)),
                pl.BlockSpec((NG, seq_len), lambda b, h: (0, b)),
                pl.BlockSpec((seq_len, 128), lambda b, h: (b, h)),
                pl.BlockSpec((1, DH), lambda b, h: (0, h))]
    args = [z, z, z, z, vt, grow, rcol, g_head]
    if has_init:
        c0, n0, layer = state
        in_specs += [pl.BlockSpec((1, 1, 2, 1, DH, DH), lambda b, h: (b, layer, 0, h, 0, 0)),
                     pl.BlockSpec((1, 1, 2, 1, 1, DH), lambda b, h: (b, layer, 0, h, 0, 0))]
        args += [c0, n0]
    in_specs.append(pl.BlockSpec(memory_space=pl.ANY))
    args.append(a_buf)
    out_specs = [pl.BlockSpec((seq_len, DH), lambda b, h: (b + off, h))]
    out_shape = [jax.ShapeDtypeStruct((NT, D), BF16)]
    if want_final:
        out_specs += [pl.BlockSpec((1, 2, 1, DH, DH), lambda b, h: (b, 0, h, 0, 0)),
                      pl.BlockSpec((1, 2, 1, 1, DH), lambda b, h: (b, 0, h, 0, 0))]
        out_shape += [jax.ShapeDtypeStruct((n_seq, 2, NH, DH, DH), F32),
                      jax.ShapeDtypeStruct((n_seq, 2, NH, 1, DH), F32)]
    res = pl.pallas_call(
        functools.partial(_mlstm_kernel, n_chunks=n_chunks, has_init=has_init, want_final=want_final),
        grid=(n_seq, NH),
        in_specs=in_specs,
        out_specs=out_specs,
        out_shape=out_shape,
        scratch_shapes=[pltpu.VMEM((2, n_chunks, DH, DH), BF16),
                        pltpu.VMEM((2, n_chunks, 8, DH), BF16),
                        pltpu.VMEM((DH, DH), F32),
                        pltpu.VMEM((8, DH), F32)],
        input_output_aliases={len(args) - 1: 0},
        compiler_params=_cp(("arbitrary", "arbitrary")),
        name="mlstm_lat" if has_init else "mlstm_ctx",
    )(*args)
    return res


def _dft_tables(seq_len):
    j = np.arange(FG)
    ang = 2.0 * np.pi * np.outer(j, j) / FG
    cs = np.concatenate([np.cos(ang), np.sin(ang)], axis=1) / np.sqrt(FG)
    return jnp.asarray(cs, F32).astype(BF16)


def _pos_dft_matrix(seq_len):
    k = lax.broadcasted_iota(jnp.int32, (seq_len, seq_len), 0)
    l = lax.broadcasted_iota(jnp.int32, (seq_len, seq_len), 1)
    ang = ((k * l) % seq_len).astype(F32) * (2.0 * np.pi / seq_len)
    sc = 1.0 / np.sqrt(seq_len)
    return jnp.concatenate([jnp.cos(ang) * sc, -jnp.sin(ang) * sc], axis=1).astype(BF16)


def _fnet_kernel(u_ref, cs_ref, w_ref, buf_ref, out_ref, pq_scr, *, seq_len):
    @pl.when(pl.program_id(1) == 0)
    def _():
        for g in range(DF // FG):
            pq = _dot(u_ref[:, g * FG:(g + 1) * FG], cs_ref[...])
            pq_scr[0:seq_len, g * FG:(g + 1) * FG] = pq[:, 0:FG].astype(BF16)
            pq_scr[seq_len:2 * seq_len, g * FG:(g + 1) * FG] = pq[:, FG:2 * FG].astype(BF16)
    out_ref[...] = _dot(w_ref[...], pq_scr[...]).astype(BF16)


def _fnet(z, yf_buf, n_seq, seq_len, tok_off):
    tr = min(seq_len, 512)
    n_rt = seq_len // tr
    off = tok_off // seq_len
    return pl.pallas_call(
        functools.partial(_fnet_kernel, seq_len=seq_len),
        grid=(n_seq, n_rt),
        in_specs=[pl.BlockSpec((seq_len, DF), lambda b, r: (b + off, ZU // DF)),
                  pl.BlockSpec((FG, 2 * FG), lambda b, r: (0, 0)),
                  pl.BlockSpec((tr, 2 * seq_len), lambda b, r: (r, 0)),
                  pl.BlockSpec(memory_space=pl.ANY)],
        out_specs=pl.BlockSpec((tr, DF), lambda b, r: ((b + off) * n_rt + r, 0)),
        out_shape=jax.ShapeDtypeStruct((NT, DF), BF16),
        scratch_shapes=[pltpu.VMEM((2 * seq_len, DF), BF16)],
        input_output_aliases={3: 0},
        compiler_params=_cp(("arbitrary", "arbitrary")),
        name="fnet",
    )(z, _dft_tables(seq_len), _pos_dft_matrix(seq_len), yf_buf)


def _router_weights(lg):
    mx = jnp.max(lg, axis=0, keepdims=True)
    ex = jnp.exp(lg - mx)
    p = ex / jnp.sum(ex, axis=0, keepdims=True)
    rows = [p[e:e + 1, :] for e in range(NE)]
    sel, score, wts = [], [], []
    for g in range(NE // EG):
        grp = rows[g * EG:(g + 1) * EG]
        sel_g = []
        for e in range(EG):
            rank = jnp.zeros_like(grp[e])
            for j in range(EG):
                if j == e:
                    continue
                beats = (grp[j] >= grp[e]) if j < e else (grp[j] > grp[e])
                rank = rank + beats.astype(F32)
            sel_g.append(rank < 2.0)
        tot = sum(jnp.where(sel_g[e], grp[e], 0.0) for e in range(EG))
        score.append(tot)
        sel.append(sel_g)
        wts.append([jnp.where(sel_g[e], grp[e] / tot, 0.0) for e in range(EG)])
    out = []
    n_groups = NE // EG
    for g in range(n_groups):
        rank = jnp.zeros_like(score[g])
        for j in range(n_groups):
            if j == g:
                continue
            beats = (score[j] >= score[g]) if j < g else (score[j] > score[g])
            rank = rank + beats.astype(F32)
        best = rank < 1.0
        out += [jnp.where(best, wts[g][e], 0.0) for e in range(EG)]
    return jnp.concatenate(out, axis=0)


def _post_kernel(a_ref, yf_ref, ga_ref, gb_ref, x_ref, mod_ref, wa_ref, wb_ref, wo_ref, g1_ref, b1_ref,
                 wr_ref, br_ref, x1_ref, h2_ref, comb_ref, *, tm):
    i = pl.program_id(0)
    ya = _dot(a_ref[...], wa_ref[...])
    yb = _dot(yf_ref[...], wb_ref[...])
    merged = _sigmoid(ga_ref[...].astype(F32)) * ya + _sigmoid(gb_ref[...].astype(F32)) * yb
    mo = _dot(merged.astype(BF16), wo_ref[...])
    gate1 = _mod_row(mod_ref, i, tm, 2)
    x1 = _layer_norm(ALPHA * x_ref[...] + gate1 * mo, g1_ref[...], b1_ref[...])
    x1_ref[...] = x1
    h2 = x1 * (1.0 + _mod_row(mod_ref, i, tm, 4)) + _mod_row(mod_ref, i, tm, 3)
    h2_ref[...] = h2.astype(BF16)
    h_hi = h2.astype(BF16)
    h_lo = (h2 - h_hi.astype(F32)).astype(BF16)
    lg2 = _dot(h_hi, wr_ref[...])
    lg = lg2[:, 0:128] + lg2[:, 128:256] + _dot(h_lo, wr_ref[:, 0:128])
    lg_t = lg.T[0:NE, :] + br_ref[...]
    comb_ref[...] = _router_weights(lg_t)


def _post(a, yf, z, x, mod, layer, wa, wb, wo, g1, b1, wr2, br_col):
    tm = 512
    full = lambda shape: pl.BlockSpec(shape, lambda i: tuple(0 for _ in shape))
    return pl.pallas_call(
        functools.partial(_post_kernel, tm=tm),
        grid=(NT // tm,),
        in_specs=[pl.BlockSpec((tm, D), lambda i: (i, 0)),
                  pl.BlockSpec((tm, DF), lambda i: (i, 0)),
                  pl.BlockSpec((tm, D), lambda i: (i, ZGA // D)),
                  pl.BlockSpec((tm, D), lambda i: (i, ZGB // D)),
                  pl.BlockSpec((tm, D), lambda i: (i, 0)),
                  pl.BlockSpec((1, MOD_ROWS, 6 * D), lambda i: (layer, 0, 0)),
                  full((D, D)), full((DF, D)), full((D, D)), full((1, D)), full((1, D)),
                  full((D, 256)), full((NE, 1))],
        out_specs=[pl.BlockSpec((tm, D), lambda i: (i, 0)),
                   pl.BlockSpec((tm, D), lambda i: (i, 0)),
                   pl.BlockSpec((NE, tm), lambda i: (0, i))],
        out_shape=[jax.ShapeDtypeStruct((NT, D), F32),
                   jax.ShapeDtypeStruct((NT, D), BF16),
                   jax.ShapeDtypeStruct((NE, NT), F32)],
        compiler_params=_cp(("arbitrary",)),
        name="post_mixer",
    )(a, yf, z, z, x, mod, wa, wb, wo, g1, b1, wr2, br_col)


def _count_kernel(comb_ref, cnt_ref):
    n = jnp.sum((comb_ref[...] > 0.0).astype(F32), axis=1, keepdims=True)
    cnt_ref[0] = jnp.broadcast_to(n, (NE, 128)).astype(jnp.int32)


def _expert_counts(comb):
    n_tiles = NT // MOE_TM
    return pl.pallas_call(
        _count_kernel,
        grid=(n_tiles,),
        in_specs=[pl.BlockSpec((NE, MOE_TM), lambda i: (0, i))],
        out_specs=pl.BlockSpec((1, NE, 128), lambda i: (i, 0, 0)),
        out_shape=jax.ShapeDtypeStruct((n_tiles, NE, 128), jnp.int32),
        compiler_params=_cp(("arbitrary",)),
        name="expert_counts",
    )(comb)


def _moe_kernel(cnt_ref, x_ref, comb_ref, tri_ref, x1_ref, mod_ref, wg_ref, wu_ref, wd_ref, g2_ref, b2_ref,
                out_ref, rank_scr, sa_scr, sb_scr, os_scr):
    i = pl.program_id(0)
    e = pl.program_id(1)
    tm = MOE_TM

    def n_tiles_of(ee):
        return (cnt_ref[i * NE + ee] + (SLOT - 1)) // SLOT

    @pl.when((i == 0) & (e == 0))
    def _():
        os_scr[...] = jnp.zeros(os_scr.shape, BF16)

    @pl.when(e == 0)
    def _():
        mask = comb_ref[...] > 0.0
        rank = _dot(mask.astype(BF16), tri_ref[...])
        rank_scr[...] = rank
        rows = lax.broadcasted_iota(jnp.int32, (NE, 1), 0)
        off_vec = jnp.zeros((NE, 1), F32)
        off = jnp.int32(0)
        for ee in range(NE):
            off_vec = jnp.where(rows == ee, off.astype(F32), off_vec)
            off = off + n_tiles_of(ee) * SLOT
        slot = off_vec + rank
        slot_a = jnp.min(jnp.where(mask, slot, 1e9), axis=0, keepdims=True)
        slot_b = jnp.max(jnp.where(mask, slot, -1.0), axis=0, keepdims=True)
        sa_scr[...] = jnp.broadcast_to(slot_a, (128, tm)).T
        sb_scr[...] = jnp.broadcast_to(slot_b, (128, tm)).T

    def offset_of(ee):
        return lax.fori_loop(0, ee, lambda j, acc: acc + n_tiles_of(j) * SLOT, jnp.int32(0))

    off_e = offset_of(e)
    rank_row = rank_scr[pl.ds(e, 1), :]
    w_row = comb_ref[pl.ds(e, 1), :]
    m_row = w_row > 0.0
    sub = lax.broadcasted_iota(jnp.int32, (SLOT, tm), 0).astype(F32)

    def slot_tile(r, carry):
        base = (r * SLOT).astype(F32)
        g = jnp.where(m_row & (rank_row == sub + base), 1.0, 0.0)
        w_slot = jnp.sum(g * w_row, axis=1, keepdims=True)
        xg = _dot(g.astype(BF16), x_ref[...]).astype(BF16)
        hg = _dot(xg, wg_ref[0])
        hu = _dot(xg, wu_ref[0])
        act = (hg * _sigmoid(hg) * hu).astype(BF16)
        o = _dot(act, wd_ref[0]) * w_slot
        os_scr[pl.ds(pl.multiple_of(off_e + r * SLOT, SLOT), SLOT), :] = o.astype(BF16)
        return carry
    lax.fori_loop(0, n_tiles_of(e), slot_tile, 0)

    @pl.when(e == NE - 1)
    def _():
        total = off_e + n_tiles_of(e) * SLOT
        n_chunks = (total + 255) // 256
        out_ref[...] = jnp.zeros(out_ref.shape, F32)
        lane = lax.broadcasted_iota(jnp.int32, (tm, 128), 1).astype(F32)
        sa = sa_scr[...]
        sb = sb_scr[...]

        def chunk(jj, carry):
            base = (jj * 256).astype(F32)
            parts = []
            for half in range(2):
                idx = lane + (base + 128.0 * half)
                parts.append(jnp.where((sa == idx) | (sb == idx), 1.0, 0.0).astype(BF16))
            gt = jnp.concatenate(parts, axis=1)
            out_ref[...] += _dot(gt, os_scr[pl.ds(pl.multiple_of(jj * 256, 256), 256), :])
            return carry
        lax.fori_loop(0, n_chunks, chunk, 0)
        gate2 = _mod_row(mod_ref, i, tm, 5)
        out_ref[...] = _layer_norm(ALPHA * x1_ref[...] + gate2 * out_ref[...], g2_ref[...], b2_ref[...])


def _moe(cnt, h2, comb, tri, x1, mod, layer, wg, wu, wd, g2, b2):
    tm = MOE_TM
    grid_spec = pltpu.PrefetchScalarGridSpec(
        num_scalar_prefetch=1,
        grid=(NT // tm, NE),
        in_specs=[pl.BlockSpec((tm, D), lambda i, e, c: (i, 0)),
                  pl.BlockSpec((NE, tm), lambda i, e, c: (0, i)),
                  pl.BlockSpec((tm, tm), lambda i, e, c: (0, 0)),
                  pl.BlockSpec((tm, D), lambda i, e, c: (i, 0)),
                  pl.BlockSpec((1, MOD_ROWS, 6 * D), lambda i, e, c: (layer, 0, 0)),
                  pl.BlockSpec((1, D, DE), lambda i, e, c: (e, 0, 0)),
                  pl.BlockSpec((1, D, DE), lambda i, e, c: (e, 0, 0)),
                  pl.BlockSpec((1, DE, D), lambda i, e, c: (e, 0, 0)),
                  pl.BlockSpec((1, D), lambda i, e, c: (0, 0)),
                  pl.BlockSpec((1, D), lambda i, e, c: (0, 0))],
        out_specs=pl.BlockSpec((tm, D), lambda i, e, c: (i, 0)),
        scratch_shapes=[pltpu.VMEM((NE, tm), F32),
                        pltpu.VMEM((tm, 128), F32),
                        pltpu.VMEM((tm, 128), F32),
                        pltpu.VMEM((S_MAX, D), BF16)])
    return pl.pallas_call(
        _moe_kernel,
        grid_spec=grid_spec,
        out_shape=jax.ShapeDtypeStruct((NT, D), F32),
        compiler_params=_cp(("arbitrary", "arbitrary")),
        name="moe",
    )(cnt, h2, comb, tri, x1, mod, wg, wu, wd, g2, b2)


def kernel(x_prompt, x_sample, state_C, state_n, state_m, c, c_ctx, w_in, b_in, w_a, w_b, w_out, g_head, ln1_g,
           ln1_b, ln2_g, ln2_b, w_ada, b_ada, w_router, b_router, w_gate, w_up, w_down):
    depth = w_in.shape[0]
    x = jnp.concatenate([x_prompt.reshape(N_CTX_TOK, D), x_sample.reshape(N_LAT_SEQ * LAT_LEN, D)], axis=0)
    cond = jnp.zeros((MOD_ROWS, D), F32).at[0].set(c_ctx).at[1:1 + N_LAT_SEQ].set(c)
    mod = _modulation(cond, w_ada, b_ada)

    wr_hi = w_router.astype(BF16)
    wr_lo = (w_router - wr_hi.astype(F32)).astype(BF16)
    pad = lambda w: jnp.pad(w, ((0, 0), (0, 128 - NE)))
    wr2 = jnp.concatenate([pad(wr_hi), pad(wr_lo)], axis=1)
    br_col = b_router.reshape(NE, 1)
    tri = jnp.triu(jnp.ones((MOE_TM, MOE_TM), BF16), k=1)
    state_n6 = state_n.reshape(N_LAT_SEQ, depth, 2, NH, 1, DH)
    zero_m0 = jnp.zeros((N_CTX_SEQ, 8, 128), F32)

    new_c, new_n, new_m = [], [], []
    for l in range(depth):
        w = w_in[l]
        w_main = jnp.concatenate([w[:, 0:4096], w[:, 4608:6656], w[:, 4096:4608]], axis=1).astype(BF16)
        bl = b_in[l]
        b_main = jnp.concatenate([bl[0:4096], bl[4608:6656], bl[4096:4608]]).reshape(1, NZ)
        wg_t = w[:, NZ:].T
        bg = bl[NZ:].reshape(NG, 1)
        wv_t = w[:, 2048:3072].T.astype(BF16)
        bv_col = bl[2048:3072].reshape(D, 1)

        h, gates_t = _prep(x, mod, l, wg_t, bg)
        z = _proj(h, w_main, b_main)
        vt = _proj_vT(h, wv_t, bv_col)

        m0_lat = jnp.broadcast_to(state_m[:, l].reshape(N_LAT_SEQ, 8, 1), (N_LAT_SEQ, 8, 128))
        grow_c, rcol_c, mfin_c = _gate_prep(gates_t, zero_m0, N_CTX_SEQ, CTX_LEN, 0)
        grow_l, rcol_l, _ = _gate_prep(gates_t, m0_lat, N_LAT_SEQ, LAT_LEN, N_CTX_TOK)

        gh = g_head[l].reshape(1, D)
        a_buf = jnp.zeros((NT, D), BF16)
        a_buf, c_fin, n_fin = _mlstm(z, vt, grow_c, rcol_c, gh, a_buf, N_CTX_SEQ, CTX_LEN, 0)
        (a_buf,) = _mlstm(z, vt, grow_l, rcol_l, gh, a_buf, N_LAT_SEQ, LAT_LEN, N_CTX_TOK,
                          state=(state_C, state_n6, l))
        new_c.append(c_fin)
        new_n.append(n_fin.reshape(N_CTX_SEQ, 2, NH, DH))
        new_m.append(mfin_c[:, :, 0].reshape(N_CTX_SEQ, 2, NH))

        yf = jnp.zeros((NT, DF), BF16)
        yf = _fnet(z, yf, N_CTX_SEQ, CTX_LEN, 0)
        yf = _fnet(z, yf, N_LAT_SEQ, LAT_LEN, N_CTX_TOK)

        x1, h2, comb = _post(a_buf, yf, z, x, mod, l, w_a[l].astype(BF16), w_b[l].astype(BF16),
                             w_out[l].astype(BF16), ln1_g[l].reshape(1, D), ln1_b[l].reshape(1, D), wr2, br_col)
        cnt = _expert_counts(comb)[:, :, 0].reshape(-1)
        x = _moe(cnt, h2, comb, tri, x1, mod, l, w_gate[l].astype(BF16), w_up[l].astype(BF16),
                 w_down[l].astype(BF16), ln2_g[l].reshape(1, D), ln2_b[l].reshape(1, D))

    y_prompt = x[:N_CTX_TOK].reshape(N_CTX_SEQ, CTX_LEN, D)
    y_sample = x[N_CTX_TOK:].reshape(N_LAT_SEQ, LAT_LEN, D)
    return (y_prompt, y_sample, jnp.stack(new_c, axis=1), jnp.stack(new_n, axis=1), jnp.stack(new_m, axis=1))
```
